```python
import jax
import jax.numpy as jnp
from jax import lax
import numpy as np

D_MODEL = 1024
BATCH = 8
SEQ = 4096
DEPTH = 1

HEAD_DIM = 64
N_SB_HEADS = 8
N_FOX_HEADS = 8
SB_WIDTH = N_SB_HEADS * HEAD_DIM
FOX_WIDTH = N_FOX_HEADS * HEAD_DIM
MIX_WIDTH = SB_WIDTH + FOX_WIDTH
IN_WIDTH = 3 * SB_WIDTH + 3 * FOX_WIDTH + N_FOX_HEADS
D_FF = 2816
BLOCK_Q = 128
N_MOD = 9
EPS = 1e-6

kernel_name = 'hymba_stickbreak_fox_macaron_adaln'


def rms_norm(x, gain):
    xf = x.astype(jnp.float32)
    y = xf * lax.rsqrt(jnp.mean(xf * xf, axis=-1, keepdims=True) + EPS)
    return (y * gain.astype(jnp.float32)).astype(x.dtype)


def modulate(h, shift, scale):
    return h * (1 + scale[:, None, :]) + shift[:, None, :]


def swiglu(h, w_gate, w_up, w_down):
    return (jax.nn.silu(h @ w_gate) * (h @ w_up)) @ w_down


def split_heads(t, n_heads):
    b, s, _ = t.shape
    return t.reshape(b, s, n_heads, HEAD_DIM).transpose(0, 2, 1, 3)


def stick_breaking_attention(q, k, v):
    seq = q.shape[2]
    scale = HEAD_DIM ** -0.5
    outs = []
    for start in range(0, seq, BLOCK_Q):
        end = start + BLOCK_Q
        z = jnp.einsum('bhqd,bhkd->bhqk', q[:, :, start:end], k[:, :, :end]).astype(jnp.float32) * scale
        mask = jnp.arange(end)[None, :] < jnp.arange(start, end)[:, None]
        log_beta = jax.nn.log_sigmoid(z)
        log_keep = jnp.where(mask, jax.nn.log_sigmoid(-z), 0.0)
        later = lax.cumsum(log_keep, axis=3, reverse=True) - log_keep
        w = jnp.where(mask, jnp.exp(log_beta + later), 0.0)
        outs.append(jnp.einsum('bhqk,bhkd->bhqd', w.astype(v.dtype), v[:, :, :end]))
    return jnp.concatenate(outs, axis=2)


def forgetting_attention(q, k, v, log_f_cum):
    seq = q.shape[2]
    scale = HEAD_DIM ** -0.5
    outs = []
    for start in range(0, seq, BLOCK_Q):
        end = start + BLOCK_Q
        z = jnp.einsum('bhqd,bhkd->bhqk', q[:, :, start:end], k[:, :, :end]).astype(jnp.float32) * scale
        z = z + log_f_cum[:, :, start:end, None] - log_f_cum[:, :, None, :end]
        mask = jnp.arange(end)[None, :] <= jnp.arange(start, end)[:, None]
        p = jax.nn.softmax(jnp.where(mask, z, -jnp.inf), axis=-1)
        outs.append(jnp.einsum('bhqk,bhkd->bhqd', p.astype(v.dtype), v[:, :, :end]))
    return jnp.concatenate(outs, axis=2)


def hybrid_mixer(h, w_in, b_f, g_q, g_k, w_o):
    proj = h @ w_in
    splits = [SB_WIDTH, 2 * SB_WIDTH, 3 * SB_WIDTH,
              3 * SB_WIDTH + FOX_WIDTH, 3 * SB_WIDTH + 2 * FOX_WIDTH, 3 * SB_WIDTH + 3 * FOX_WIDTH]
    sb_q, sb_k, sb_v, fox_q, fox_k, fox_v, fox_f = jnp.split(proj, splits, axis=-1)
    sb_out = stick_breaking_attention(split_heads(sb_q, N_SB_HEADS), split_heads(sb_k, N_SB_HEADS),
                                      split_heads(sb_v, N_SB_HEADS))
    fq = rms_norm(split_heads(fox_q, N_FOX_HEADS), g_q[None, :, None, :])
    fk = rms_norm(split_heads(fox_k, N_FOX_HEADS), g_k[None, :, None, :])
    log_f = jax.nn.log_sigmoid((fox_f + b_f).astype(jnp.float32))
    log_f_cum = jnp.cumsum(log_f, axis=1).transpose(0, 2, 1)
    fox_out = forgetting_attention(fq, fk, split_heads(fox_v, N_FOX_HEADS), log_f_cum)
    heads = jnp.concatenate([sb_out, fox_out], axis=1)
    b, _, s, _ = heads.shape
    return heads.transpose(0, 2, 1, 3).reshape(b, s, MIX_WIDTH) @ w_o


def setup_inputs(seed: int = 0) -> dict:
    key = jax.random.key(seed)
    ks = jax.random.split(key, 20)
    nrm = jax.random.normal
    d_s = D_MODEL ** -0.5
    return {
        'x': nrm(ks[0], (BATCH, SEQ, D_MODEL), jnp.float32),
        'c': nrm(ks[1], (BATCH, D_MODEL), jnp.float32),
        'w_mod': nrm(ks[2], (DEPTH, D_MODEL, N_MOD * D_MODEL), jnp.float32) * (0.1 * d_s),
        'b_mod': nrm(ks[3], (DEPTH, N_MOD * D_MODEL), jnp.float32) * 0.01,
        'g_ffn1': 1.0 + 0.02 * nrm(ks[4], (DEPTH, D_MODEL), jnp.float32),
        'w1_gate': nrm(ks[5], (DEPTH, D_MODEL, D_FF), jnp.float32) * d_s,
        'w1_up': nrm(ks[6], (DEPTH, D_MODEL, D_FF), jnp.float32) * d_s,
        'w1_down': nrm(ks[7], (DEPTH, D_FF, D_MODEL), jnp.float32) * D_FF ** -0.5,
        'g_mix': 1.0 + 0.02 * nrm(ks[8], (DEPTH, D_MODEL), jnp.float32),
        'w_in': nrm(ks[9], (DEPTH, D_MODEL, IN_WIDTH), jnp.float32) * d_s,
        'b_f': jax.random.uniform(ks[10], (DEPTH, N_FOX_HEADS), jnp.float32, 1.0, 4.0),
        'g_q': 1.0 + 0.02 * nrm(ks[11], (DEPTH, N_FOX_HEADS, HEAD_DIM), jnp.float32),
        'g_k': 1.0 + 0.02 * nrm(ks[12], (DEPTH, N_FOX_HEADS, HEAD_DIM), jnp.float32),
        'w_o': nrm(ks[13], (DEPTH, MIX_WIDTH, D_MODEL), jnp.float32) * MIX_WIDTH ** -0.5,
        'g_ffn2': 1.0 + 0.02 * nrm(ks[14], (DEPTH, D_MODEL), jnp.float32),
        'w2_gate': nrm(ks[15], (DEPTH, D_MODEL, D_FF), jnp.float32) * d_s,
        'w2_up': nrm(ks[16], (DEPTH, D_MODEL, D_FF), jnp.float32) * d_s,
        'w2_down': nrm(ks[17], (DEPTH, D_FF, D_MODEL), jnp.float32) * D_FF ** -0.5,
    }


def reference(x, c, w_mod, b_mod, g_ffn1, w1_gate, w1_up, w1_down, g_mix, w_in, b_f, g_q, g_k,
              w_o, g_ffn2, w2_gate, w2_up, w2_down):
    c_act = jax.nn.silu(c)
    for l in range(DEPTH):
        mod = c_act @ w_mod[l] + b_mod[l]
        sh1, sc1, ga1, sh2, sc2, ga2, sh3, sc3, ga3 = jnp.split(mod, N_MOD, axis=-1)
        h = modulate(rms_norm(x, g_ffn1[l]), sh1, sc1)
        x = x + 0.5 * (1 + ga1[:, None, :]) * swiglu(h, w1_gate[l], w1_up[l], w1_down[l])
        h = modulate(rms_norm(x, g_mix[l]), sh2, sc2)
        x = x + (1 + ga2[:, None, :]) * hybrid_mixer(h, w_in[l], b_f[l], g_q[l], g_k[l], w_o[l])
        h = modulate(rms_norm(x, g_ffn2[l]), sh3, sc3)
        x = x + 0.5 * (1 + ga3[:, None, :]) * swiglu(h, w2_gate[l], w2_up[l], w2_down[l])
    return x
```

```python
import functools

import jax
import jax.numpy as jnp
from jax import lax
from jax.experimental import pallas as pl
from jax.experimental.pallas import tpu as pltpu

F32 = jnp.float32
BF16 = jnp.bfloat16

HEAD_DIM = 64
N_SB_HEADS = 8
N_FOX_HEADS = 8
SB_WIDTH = N_SB_HEADS * HEAD_DIM
FOX_WIDTH = N_FOX_HEADS * HEAD_DIM
N_MOD = 9
EPS = 1e-6
LANES = 128
PAIR = LANES // HEAD_DIM

VMEM_LIMIT = 56 * 1024 * 1024

SB_LOG_CUTOFF = -60.0


def _cparams(sem):
    return pltpu.CompilerParams(dimension_semantics=sem, vmem_limit_bytes=VMEM_LIMIT)


def _log_sigmoid(x):
    return jnp.minimum(x, 0.0) - jnp.log1p(jnp.exp(-jnp.abs(x)))


def _split2(x):
    hi = x.astype(BF16)
    lo = (x - hi.astype(F32)).astype(BF16)
    return hi, lo


def _split3(x):
    hi = x.astype(BF16)
    r = x - hi.astype(F32)
    mid = r.astype(BF16)
    lo = (r - mid.astype(F32)).astype(BF16)
    return hi, mid, lo


def _dot(a, b):
    return jnp.dot(a, b, preferred_element_type=F32)


def _dot_nt(a, b):
    return lax.dot_general(a, b, (((1,), (1,)), ((), ())), preferred_element_type=F32)


def _norm_modulate(x, g, shift, scale):
    ms = jnp.mean(x * x, axis=-1, keepdims=True)
    y = x * lax.rsqrt(ms + EPS) * g
    return y * (1.0 + scale) + shift


def _mod_kernel(c_ref, w_ref, b_ref, o_ref):
    c = c_ref[...]
    c_act = c * jax.nn.sigmoid(c)
    o_ref[...] = jnp.dot(c_act, w_ref[...], preferred_element_type=F32,
                         precision=lax.Precision.HIGHEST) + b_ref[...]


def _modulation(c, w_mod, b_mod):
    batch, d = c.shape
    n = w_mod.shape[1]
    tn = d
    return pl.pallas_call(
        _mod_kernel,
        out_shape=jax.ShapeDtypeStruct((batch, n), F32),
        grid=(n // tn,),
        in_specs=[
            pl.BlockSpec((batch, d), lambda j: (0, 0)),
            pl.BlockSpec((d, tn), lambda j: (0, j)),
            pl.BlockSpec((1, tn), lambda j: (0, j)),
        ],
        out_specs=pl.BlockSpec((batch, tn), lambda j: (0, j)),
        compiler_params=_cparams(("arbitrary",)),
        name="mod",
    )(c, w_mod, b_mod.reshape(1, n))


FFN_TM = 512
FFN_CHUNK = 256


def _ffn_kernel(x_ref, mod_ref, g_ref, wg_ref, wu_ref, wd_ref, o_ref, h_ref, *, mod_base):
    x = x_ref[...]
    shift = mod_ref[0, mod_base:mod_base + 1, :]
    scale = mod_ref[0, mod_base + 1:mod_base + 2, :]
    gate = mod_ref[0, mod_base + 2:mod_base + 3, :]
    hb = _norm_modulate(x, g_ref[...], shift, scale).astype(BF16)
    d_ff = wg_ref.shape[1]
    for c in range(d_ff // FFN_CHUNK):
        cols = slice(c * FFN_CHUNK, (c + 1) * FFN_CHUNK)
        a = _dot(hb, wg_ref[:, cols])
        u = _dot(hb, wu_ref[:, cols])
        h_ref[:, cols] = (a * jax.nn.sigmoid(a) * u).astype(BF16)
    y = _dot(h_ref[...], wd_ref[...])
    o_ref[...] = x + (0.5 * (1.0 + gate)) * y


def _ffn(x2d, mod3, g, wg, wu, wd, *, mod_base, seq):
    t, d = x2d.shape
    d_ff = wg.shape[1]
    tm = FFN_TM
    per_batch = seq // tm
    const = lambda i: (0, 0)
    return pl.pallas_call(
        functools.partial(_ffn_kernel, mod_base=mod_base),
        out_shape=jax.ShapeDtypeStruct((t, d), F32),
        grid=(t // tm,),
        in_specs=[
            pl.BlockSpec((tm, d), lambda i: (i, 0)),
            pl.BlockSpec((1, N_MOD, d), lambda i: (i // per_batch, 0, 0)),
            pl.BlockSpec((1, d), const),
            pl.BlockSpec((d, d_ff), const),
            pl.BlockSpec((d, d_ff), const),
            pl.BlockSpec((d_ff, d), const),
        ],
        out_specs=pl.BlockSpec((tm, d), lambda i: (i, 0)),
        scratch_shapes=[pltpu.VMEM((tm, d_ff), BF16)],
        compiler_params=_cparams(("arbitrary",)),
        name=f"ffn{mod_base}",
    )(x2d, mod3, g, wg, wu, wd)


PROJ_TM = 512
NORM_GROUP = 256


def _head_group_sumsq(y):
    r = lax.broadcasted_iota(jnp.int32, (NORM_GROUP, NORM_GROUP), 0) // HEAD_DIM
    c = lax.broadcasted_iota(jnp.int32, (NORM_GROUP, NORM_GROUP), 1) // HEAD_DIM
    bd = jnp.where(r == c, 1.0, 0.0).astype(BF16)
    hi, lo = _split2(y * y)
    parts = []
    for g in range(y.shape[1] // NORM_GROUP):
        cols = slice(g * NORM_GROUP, (g + 1) * NORM_GROUP)
        parts.append(_dot(hi[:, cols], bd) + _dot(lo[:, cols], bd))
    return jnp.concatenate(parts, axis=1)


def _inproj_kernel(x_ref, mod_ref, g_ref, w_ref, wf_ref, bf_ref, gq_ref, gk_ref,
                   sbq_ref, sbk_ref, sbv_ref, fq_ref, fk_ref, fv_ref, fcum_ref, carry_ref):
    tm = x_ref.shape[1]
    x = x_ref[0]
    shift = mod_ref[0, 3:4, :]
    scale = mod_ref[0, 4:5, :]
    hb = _norm_modulate(x, g_ref[...], shift, scale).astype(BF16)

    def proj(c):
        return _dot(hb, w_ref[:, c * SB_WIDTH:(c + 1) * SB_WIDTH])

    qk_scale = HEAD_DIM ** -0.5
    sbq_ref[0] = (proj(0) * qk_scale).astype(BF16)
    sbk_ref[0] = proj(1).astype(BF16)
    sbv_ref[0] = proj(2).astype(BF16)

    y = proj(3)
    ms = _head_group_sumsq(y) * (1.0 / HEAD_DIM)
    fq_ref[0] = (y * lax.rsqrt(ms + EPS) * gq_ref[...] * qk_scale).astype(BF16)
    y = proj(4)
    ms = _head_group_sumsq(y) * (1.0 / HEAD_DIM)
    fk_ref[0] = (y * lax.rsqrt(ms + EPS) * gk_ref[...]).astype(BF16)
    fv_ref[0] = proj(5).astype(BF16)

    @pl.when(pl.program_id(1) == 0)
    def _():
        carry_ref[...] = jnp.zeros_like(carry_ref)

    logf = _log_sigmoid(_dot(hb, wf_ref[...]) + bf_ref[...])
    r = lax.broadcasted_iota(jnp.int32, (tm, tm), 0)
    c = lax.broadcasted_iota(jnp.int32, (tm, tm), 1)
    tri = jnp.where(r >= c, 1.0, 0.0).astype(BF16)
    hi, mid, lo = _split3(logf)
    cum = _dot(tri, hi) + _dot(tri, mid) + _dot(tri, lo) + carry_ref[...]
    carry_ref[...] = cum[tm - 1:tm, :]
    fcum_ref[0] = cum.T[:N_FOX_HEADS, :]


def _inproj(x3, mod3, g, w_main, w_f, b_f, gq, gk):
    batch, seq, d = x3.shape
    tm = PROJ_TM
    const2 = lambda b, i: (0, 0)
    tile = lambda b, i: (b, i, 0)
    qkv = jax.ShapeDtypeStruct((batch, seq, SB_WIDTH), BF16)
    return pl.pallas_call(
        _inproj_kernel,
        out_shape=(qkv,) * 6 + (jax.ShapeDtypeStruct((batch, N_FOX_HEADS, seq), F32),),
        grid=(batch, seq // tm),
        in_specs=[
            pl.BlockSpec((1, tm, d), tile),
            pl.BlockSpec((1, N_MOD, d), lambda b, i: (b, 0, 0)),
            pl.BlockSpec((1, d), const2),
            pl.BlockSpec(w_main.shape, const2),
            pl.BlockSpec(w_f.shape, const2),
            pl.BlockSpec((1, LANES), const2),
            pl.BlockSpec((1, FOX_WIDTH), const2),
            pl.BlockSpec((1, FOX_WIDTH), const2),
        ],
        out_specs=tuple(pl.BlockSpec((1, tm, SB_WIDTH), tile) for _ in range(6))
        + (pl.BlockSpec((1, N_FOX_HEADS, tm), lambda b, i: (b, 0, i)),),
        scratch_shapes=[pltpu.VMEM((1, LANES), F32)],
        compiler_params=_cparams(("arbitrary", "arbitrary")),
        name="inproj",
    )(x3, mod3, g, w_main, w_f, b_f, gq, gk)


SB_T = 128


def _sb_kernel(q_ref, k_ref, v_ref, o_ref, acc_ref):
    t = SB_T
    i = pl.program_id(1)
    lane = lax.broadcasted_iota(jnp.int32, (1, LANES), 1)
    first_half = lane < HEAD_DIM
    row = lax.broadcasted_iota(jnp.int32, (t, t), 0)
    col = lax.broadcasted_iota(jnp.int32, (t, t), 1)
    causal = col < row
    tri = jnp.where(row > col, 1.0, 0.0).astype(BF16)

    qs = []
    for p in range(N_SB_HEADS // PAIR):
        q2 = q_ref[0, :, p * LANES:(p + 1) * LANES]
        zero = jnp.zeros_like(q2)
        qs.append(jnp.where(first_half, q2, zero))
        qs.append(jnp.where(first_half, zero, q2))

    def block(j, run, diagonal):
        start = pl.multiple_of(j * t, t)
        new_run = []
        for h in range(N_SB_HEADS):
            p = h // PAIR
            k2 = k_ref[0, pl.ds(start, t), p * LANES:(p + 1) * LANES]
            v2 = v_ref[0, pl.ds(start, t), p * LANES:(p + 1) * LANES]
            z = _dot_nt(qs[h], k2)
            soft = jnp.log1p(jnp.exp(-jnp.abs(z)))
            log_beta = jnp.minimum(z, 0.0) - soft
            log_keep = -jnp.maximum(z, 0.0) - soft
            if diagonal:
                log_keep = jnp.where(causal, log_keep, 0.0)
            hi, lo = _split2(log_keep)
            later = _dot(hi, tri) + _dot(lo, tri)
            w = jnp.exp(log_beta + later + run[h])
            if diagonal:
                w = jnp.where(causal, w, 0.0)
            pv = _dot(w.astype(BF16), v2)
            if diagonal:
                acc_ref[h] = pv
            else:
                acc_ref[h] += pv
            new_run.append(run[h] + later[:, 0:1] + log_keep[:, 0:1])
        return new_run

    run = block(i, [jnp.zeros((t, 1), F32)] * N_SB_HEADS, True)

    def live(run):
        m = run[0]
        for r in run[1:]:
            m = jnp.maximum(m, r)
        return jnp.max(m) > SB_LOG_CUTOFF

    def cond(carry):
        j, go = carry[0], carry[1]
        return jnp.logical_and(j >= 0, go)

    def body(carry):
        j = carry[0]
        run = block(j, list(carry[2:]), False)
        return (j - 1, live(run), *run)

    lax.while_loop(cond, body, (i - 1, live(run), *run))

    for p in range(N_SB_HEADS // PAIR):
        o_ref[0, :, p * LANES:(p + 1) * LANES] = jnp.where(
            first_half, acc_ref[PAIR * p], acc_ref[PAIR * p + 1]).astype(o_ref.dtype)


def _sb_attention(q, k, v):
    batch, seq, width = q.shape
    t = SB_T
    full = lambda b, i: (b, 0, 0)
    tile = lambda b, i: (b, i, 0)
    return pl.pallas_call(
        _sb_kernel,
        out_shape=jax.ShapeDtypeStruct((batch, seq, width), BF16),
        grid=(batch, seq // t),
        in_specs=[
            pl.BlockSpec((1, t, width), tile),
            pl.BlockSpec((1, seq, width), full),
            pl.BlockSpec((1, seq, width), full),
        ],
        out_specs=pl.BlockSpec((1, t, width), tile),
        scratch_shapes=[pltpu.VMEM((N_SB_HEADS, t, LANES), F32)],
        compiler_params=_cparams(("arbitrary", "arbitrary")),
        name="sb",
    )(q, k, v)


FOX_T = 256
NEG_BIG = -1e30


def _fox_kernel(q_ref, k_ref, v_ref, f_ref, o_ref, acc_ref):
    t = FOX_T
    i = pl.program_id(2)
    lane = lax.broadcasted_iota(jnp.int32, (1, LANES), 1)
    first_half = lane < HEAD_DIM
    row = lax.broadcasted_iota(jnp.int32, (t, t), 0)
    col = lax.broadcasted_iota(jnp.int32, (t, t), 1)
    causal = col <= row

    q2 = q_ref[0]
    zero = jnp.zeros_like(q2)
    qs = [jnp.where(first_half, q2, zero), jnp.where(first_half, zero, q2)]
    acc_ref[...] = jnp.zeros_like(acc_ref)

    def block(j, carry, diagonal):
        start = pl.multiple_of(j * t, t)
        k2 = k_ref[0, pl.ds(start, t), :]
        v2 = v_ref[0, pl.ds(start, t), :]
        out = []
        for h in range(PAIR):
            m, l = carry[2 * h], carry[2 * h + 1]
            s = _dot_nt(qs[h], k2) - f_ref[0, 0, h:h + 1, pl.ds(start, t)]
            if diagonal:
                s = jnp.where(causal, s, NEG_BIG)
            m_new = jnp.maximum(m, jnp.max(s, axis=-1, keepdims=True))
            alpha = jnp.exp(m - m_new)
            pr = jnp.exp(s - m_new)
            l_new = alpha * l + jnp.sum(pr, axis=-1, keepdims=True)
            acc_ref[h] = alpha * acc_ref[h] + _dot(pr.astype(BF16), v2)
            out += [m_new, l_new]
        return tuple(out)

    init = (jnp.full((t, 1), NEG_BIG, F32), jnp.zeros((t, 1), F32)) * PAIR
    carry = lax.fori_loop(0, i, lambda j, c: block(j, c, False), init)
    carry = block(i, carry, True)
    o_ref[0] = jnp.where(first_half, acc_ref[0] / carry[1], acc_ref[1] / carry[3]).astype(o_ref.dtype)


def _fox_attention(q, k, v, fcum):
    batch, seq, width = q.shape
    t = FOX_T
    n_pairs = width // LANES
    f4 = fcum.reshape(batch, n_pairs, PAIR, seq)
    return pl.pallas_call(
        _fox_kernel,
        out_shape=jax.ShapeDtypeStruct((batch, seq, width), BF16),
        grid=(batch, n_pairs, seq // t),
        in_specs=[
            pl.BlockSpec((1, t, LANES), lambda b, p, i: (b, i, p)),
            pl.BlockSpec((1, seq, LANES), lambda b, p, i: (b, 0, p)),
            pl.BlockSpec((1, seq, LANES), lambda b, p, i: (b, 0, p)),
            pl.BlockSpec((1, 1, PAIR, seq), lambda b, p, i: (b, p, 0, 0)),
        ],
        out_specs=pl.BlockSpec((1, t, LANES), lambda b, p, i: (b, i, p)),
        scratch_shapes=[pltpu.VMEM((PAIR, t, LANES), F32)],
        compiler_params=_cparams(("arbitrary", "arbitrary", "arbitrary")),
        name="fox",
    )(q, k, v, f4)


OUT_TM = 512


def _outproj_kernel(x_ref, sb_ref, fox_ref, mod_ref, wsb_ref, wfox_ref, o_ref):
    gate = mod_ref[0, 5:6, :]
    y = _dot(sb_ref[...], wsb_ref[...]) + _dot(fox_ref[...], wfox_ref[...])
    o_ref[...] = x_ref[...] + (1.0 + gate) * y


def _outproj(x2d, sb2d, fox2d, mod3, w_sb, w_fox, *, seq):
    t, d = x2d.shape
    tm = OUT_TM
    per_batch = seq // tm
    const = lambda i: (0, 0)
    tile = lambda i: (i, 0)
    return pl.pallas_call(
        _outproj_kernel,
        out_shape=jax.ShapeDtypeStruct((t, d), F32),
        grid=(t // tm,),
        in_specs=[
            pl.BlockSpec((tm, d), tile),
            pl.BlockSpec((tm, SB_WIDTH), tile),
            pl.BlockSpec((tm, FOX_WIDTH), tile),
            pl.BlockSpec((1, N_MOD, d), lambda i: (i // per_batch, 0, 0)),
            pl.BlockSpec(w_sb.shape, const),
            pl.BlockSpec(w_fox.shape, const),
        ],
        out_specs=pl.BlockSpec((tm, d), tile),
        compiler_params=_cparams(("arbitrary",)),
        name="outproj",
    )(x2d, sb2d, fox2d, mod3, w_sb, w_fox)


def kernel(x, c, w_mod, b_mod, g_ffn1, w1_gate, w1_up, w1_down, g_mix, w_in, b_f, g_q, g_k,
           w_o, g_ffn2, w2_gate, w2_up, w2_down):
    batch, seq, d = x.shape
    depth = w_mod.shape[0]
    qkv_width = 3 * SB_WIDTH + 3 * FOX_WIDTH
    xf = x.reshape(batch * seq, d)
    for l in range(depth):
        mod3 = _modulation(c, w_mod[l], b_mod[l]).reshape(batch, N_MOD, d)
        xf = _ffn(xf, mod3, g_ffn1[l].reshape(1, d), w1_gate[l].astype(BF16), w1_up[l].astype(BF16),
                  w1_down[l].astype(BF16), mod_base=0, seq=seq)

        w_main = w_in[l][:, :qkv_width].astype(BF16)
        w_f = jnp.pad(w_in[l][:, qkv_width:], ((0, 0), (0, LANES - N_FOX_HEADS))).astype(BF16)
        b_f_pad = jnp.pad(b_f[l], (0, LANES - N_FOX_HEADS)).reshape(1, LANES)
        sbq, sbk, sbv, fq, fk, fv, fcum = _inproj(
            xf.reshape(batch, seq, d), mod3, g_mix[l].reshape(1, d), w_main, w_f, b_f_pad,
            g_q[l].reshape(1, FOX_WIDTH), g_k[l].reshape(1, FOX_WIDTH))
        sb_out = _sb_attention(sbq, sbk, sbv)
        fox_out = _fox_attention(fq, fk, fv, fcum)
        w_o_b = w_o[l].astype(BF16)
        xf = _outproj(xf, sb_out.reshape(batch * seq, SB_WIDTH), fox_out.reshape(batch * seq, FOX_WIDTH),
                      mod3, w_o_b[:SB_WIDTH], w_o_b[SB_WIDTH:], seq=seq)

        xf = _ffn(xf, mod3, g_ffn2[l].reshape(1, d), w2_gate[l].astype(BF16), w2_up[l].astype(BF16),
                  w2_down[l].astype(BF16), mod_base=6, seq=seq)
    return xf.reshape(batch, seq, d)
```

```python
import functools

import jax
import jax.numpy as jnp
from jax import lax
from jax.experimental import pallas as pl
from jax.experimental.pallas import tpu as pltpu

F32 = jnp.float32
BF16 = jnp.bfloat16

HEAD_DIM = 64
N_SB_HEADS = 8
N_FOX_HEADS = 8
SB_WIDTH = N_SB_HEADS * HEAD_DIM
FOX_WIDTH = N_FOX_HEADS * HEAD_DIM
N_MOD = 9
EPS = 1e-6
LOG2E = 1.4426950408889634
LANES = 128
PAIR = LANES // HEAD_DIM

VMEM_LIMIT = 56 * 1024 * 1024

SB_LOG2_CUTOFF = -60.0 * LOG2E


def _cparams(sem):
    return pltpu.CompilerParams(dimension_semantics=sem, vmem_limit_bytes=VMEM_LIMIT)


def _log_sigmoid(x):
    return jnp.minimum(x, 0.0) - jnp.log1p(jnp.exp(-jnp.abs(x)))


def _split2(x):
    hi = x.astype(BF16)
    lo = (x - hi.astype(F32)).astype(BF16)
    return hi, lo


def _split3(x):
    hi = x.astype(BF16)
    r = x - hi.astype(F32)
    mid = r.astype(BF16)
    lo = (r - mid.astype(F32)).astype(BF16)
    return hi, mid, lo


def _dot(a, b):
    return jnp.dot(a, b, preferred_element_type=F32)


def _dot_nt(a, b):
    return lax.dot_general(a, b, (((1,), (1,)), ((), ())), preferred_element_type=F32)


def _norm_modulate(x, g, shift, scale):
    ms = jnp.mean(x * x, axis=-1, keepdims=True)
    y = x * lax.rsqrt(ms + EPS) * g
    return y * (1.0 + scale) + shift


def _mod_kernel(c_ref, w_ref, b_ref, o_ref):
    c = c_ref[...]
    c_act = c * jax.nn.sigmoid(c)
    o_ref[...] = jnp.dot(c_act, w_ref[...], preferred_element_type=F32,
                         precision=lax.Precision.HIGHEST) + b_ref[...]


def _modulation(c, w_mod, b_mod):
    batch, d = c.shape
    n = w_mod.shape[1]
    tn = d
    return pl.pallas_call(
        _mod_kernel,
        out_shape=jax.ShapeDtypeStruct((batch, n), F32),
        grid=(n // tn,),
        in_specs=[
            pl.BlockSpec((batch, d), lambda j: (0, 0)),
            pl.BlockSpec((d, tn), lambda j: (0, j)),
            pl.BlockSpec((1, tn), lambda j: (0, j)),
        ],
        out_specs=pl.BlockSpec((batch, tn), lambda j: (0, j)),
        compiler_params=_cparams(("arbitrary",)),
        name="mod",
    )(c, w_mod, b_mod.reshape(1, n))


FFN_TM = 512
FFN_CHUNK = 256


def _ffn_kernel(x_ref, mod_ref, g_ref, wg_ref, wu_ref, wd_ref, o_ref, h_ref, *, mod_base):
    x = x_ref[...]
    shift = mod_ref[0, mod_base:mod_base + 1, :]
    scale = mod_ref[0, mod_base + 1:mod_base + 2, :]
    gate = mod_ref[0, mod_base + 2:mod_base + 3, :]
    hb = _norm_modulate(x, g_ref[...], shift, scale).astype(BF16)
    d_ff = wg_ref.shape[1]
    for c in range(d_ff // FFN_CHUNK):
        cols = slice(c * FFN_CHUNK, (c + 1) * FFN_CHUNK)
        a = _dot(hb, wg_ref[:, cols])
        u = _dot(hb, wu_ref[:, cols])
        h_ref[:, cols] = (a * jax.nn.sigmoid(a) * u).astype(BF16)
    y = _dot(h_ref[...], wd_ref[...])
    o_ref[...] = x + (0.5 * (1.0 + gate)) * y


def _ffn(x2d, mod3, g, wg, wu, wd, *, mod_base, seq):
    t, d = x2d.shape
    d_ff = wg.shape[1]
    tm = FFN_TM
    per_batch = seq // tm
    const = lambda i: (0, 0)
    return pl.pallas_call(
        functools.partial(_ffn_kernel, mod_base=mod_base),
        out_shape=jax.ShapeDtypeStruct((t, d), F32),
        grid=(t // tm,),
        in_specs=[
            pl.BlockSpec((tm, d), lambda i: (i, 0)),
            pl.BlockSpec((1, N_MOD, d), lambda i: (i // per_batch, 0, 0)),
            pl.BlockSpec((1, d), const),
            pl.BlockSpec((d, d_ff), const),
            pl.BlockSpec((d, d_ff), const),
            pl.BlockSpec((d_ff, d), const),
        ],
        out_specs=pl.BlockSpec((tm, d), lambda i: (i, 0)),
        scratch_shapes=[pltpu.VMEM((tm, d_ff), BF16)],
        compiler_params=_cparams(("arbitrary",)),
        name=f"ffn{mod_base}",
    )(x2d, mod3, g, wg, wu, wd)


PROJ_TM = 512
NORM_GROUP = 256


def _head_group_sumsq(y):
    r = lax.broadcasted_iota(jnp.int32, (NORM_GROUP, NORM_GROUP), 0) // HEAD_DIM
    c = lax.broadcasted_iota(jnp.int32, (NORM_GROUP, NORM_GROUP), 1) // HEAD_DIM
    bd = jnp.where(r == c, 1.0, 0.0).astype(BF16)
    hi, lo = _split2(y * y)
    parts = []
    for g in range(y.shape[1] // NORM_GROUP):
        cols = slice(g * NORM_GROUP, (g + 1) * NORM_GROUP)
        parts.append(_dot(hi[:, cols], bd) + _dot(lo[:, cols], bd))
    return jnp.concatenate(parts, axis=1)


def _inproj_kernel(x_ref, mod_ref, g_ref, w_ref, wf_ref, bf_ref, gq_ref, gk_ref,
                   sbq_ref, sbk_ref, sbv_ref, fqt_ref, fk_ref, fvt_ref, faug_ref, carry_ref):
    tm = x_ref.shape[1]
    x = x_ref[0]
    shift = mod_ref[0, 3:4, :]
    scale = mod_ref[0, 4:5, :]
    hb = _norm_modulate(x, g_ref[...], shift, scale).astype(BF16)

    def proj(c):
        return _dot(hb, w_ref[:, c * SB_WIDTH:(c + 1) * SB_WIDTH])

    qk_scale = HEAD_DIM ** -0.5
    sbq_ref[0] = (proj(0) * (qk_scale * LOG2E)).T.astype(BF16)
    sbk_ref[0] = proj(1).astype(BF16)
    sbv_ref[0] = proj(2).T.astype(BF16)

    y = proj(3)
    ms = _head_group_sumsq(y) * (1.0 / HEAD_DIM)
    fq = y * lax.rsqrt(ms + EPS) * gq_ref[...] * (qk_scale * LOG2E)
    fqt_ref[0] = fq.T.astype(BF16)
    y = proj(4)
    ms = _head_group_sumsq(y) * (1.0 / HEAD_DIM)
    fk_ref[0] = (y * lax.rsqrt(ms + EPS) * gk_ref[...]).astype(BF16)
    fvt_ref[0] = proj(5).T.astype(BF16)

    @pl.when(pl.program_id(1) == 0)
    def _():
        carry_ref[...] = jnp.zeros_like(carry_ref)

    logf = _log_sigmoid(_dot(hb, wf_ref[...]) + bf_ref[...])
    r = lax.broadcasted_iota(jnp.int32, (tm, tm), 0)
    c = lax.broadcasted_iota(jnp.int32, (tm, tm), 1)
    tri = jnp.where(r >= c, 1.0, 0.0).astype(BF16)
    hi, mid, lo = _split3(logf)
    cum = _dot(tri, hi) + _dot(tri, mid) + _dot(tri, lo) + carry_ref[...]
    carry_ref[...] = cum[tm - 1:tm, :]
    hi, mid, lo = _split3(cum * (-LOG2E))
    lane = lax.broadcasted_iota(jnp.int32, (1, LANES), 1)
    piece = jnp.where(lane % 3 == 0, hi, jnp.where(lane % 3 == 1, mid, lo))
    faug_ref[0] = jnp.where(lane < 3 * N_FOX_HEADS, piece, jnp.zeros_like(piece))


def _inproj(x3, mod3, g, w_main, w_f, b_f, gq, gk):
    batch, seq, d = x3.shape
    tm = PROJ_TM
    const2 = lambda b, i: (0, 0)
    tile = lambda b, i: (b, i, 0)
    row_major = jax.ShapeDtypeStruct((batch, seq, SB_WIDTH), BF16)
    transposed = jax.ShapeDtypeStruct((batch, SB_WIDTH, seq), BF16)
    rm_spec = pl.BlockSpec((1, tm, SB_WIDTH), tile)
    t_spec = pl.BlockSpec((1, SB_WIDTH, tm), lambda b, i: (b, 0, i))
    return pl.pallas_call(
        _inproj_kernel,
        out_shape=(transposed, row_major, transposed, transposed, row_major, transposed,
                   jax.ShapeDtypeStruct((batch, seq, LANES), BF16)),
        grid=(batch, seq // tm),
        in_specs=[
            pl.BlockSpec((1, tm, d), tile),
            pl.BlockSpec((1, N_MOD, d), lambda b, i: (b, 0, 0)),
            pl.BlockSpec((1, d), const2),
            pl.BlockSpec(w_main.shape, const2),
            pl.BlockSpec(w_f.shape, const2),
            pl.BlockSpec((1, LANES), const2),
            pl.BlockSpec((1, FOX_WIDTH), const2),
            pl.BlockSpec((1, FOX_WIDTH), const2),
        ],
        out_specs=(t_spec, rm_spec, t_spec, t_spec, rm_spec, t_spec,
                   pl.BlockSpec((1, tm, LANES), tile)),
        scratch_shapes=[pltpu.VMEM((1, LANES), F32)],
        compiler_params=_cparams(("arbitrary", "arbitrary")),
        name="inproj",
    )(x3, mod3, g, w_main, w_f, b_f, gq, gk)


SB_TQ = 256
SB_TK = 128


def _sb_kernel(qt_ref, k_ref, vt_ref, o_ref, sc_ref, lb_ref, hl_ref, acc_ref):
    tq, tk = SB_TQ, SB_TK
    i = pl.program_id(1)
    key = lax.broadcasted_iota(jnp.int32, (tk, tq), 0)
    qry = lax.broadcasted_iota(jnp.int32, (tk, tq), 1)
    older = lax.broadcasted_iota(jnp.int32, (tk, tk), 0)
    newer = lax.broadcasted_iota(jnp.int32, (tk, tk), 1)
    tri = jnp.where(newer > older, 1.0, 0.0).astype(BF16)
    tri2 = jnp.concatenate([tri, tri], axis=1)
    dim = lax.broadcasted_iota(jnp.int32, (LANES, tq), 0)

    q_m = []
    for p in range(N_SB_HEADS // PAIR):
        qt = qt_ref[0, p * LANES:(p + 1) * LANES, :].astype(F32)
        for h in range(PAIR):
            q_m.append(jnp.where(dim // HEAD_DIM == h, qt, 0.0).astype(BF16))
    acc_ref[...] = jnp.zeros_like(acc_ref)

    def block(j, run, key_offset):
        start = pl.multiple_of(j * tk, tk)
        visible = None if key_offset is None else key + key_offset < qry
        for h in range(N_SB_HEADS):
            p = h // PAIR
            sc_ref[h] = _dot(k_ref[0, pl.ds(start, tk), p * LANES:(p + 1) * LANES], q_m[h])
        newest = []
        for h in range(N_SB_HEADS):
            z = sc_ref[h]
            soft = jnp.log2(1.0 + jnp.exp2(-jnp.abs(z)))
            log_beta = jnp.minimum(z, 0.0) - soft
            log_keep = log_beta - z
            if visible is not None:
                log_keep = jnp.where(visible, log_keep, 0.0)
            lb_ref[h] = log_beta
            hi, lo = _split2(log_keep)
            hl_ref[h, :tk, :] = hi
            hl_ref[h, tk:, :] = lo
            newest.append(log_keep[0:1, :])
        new_run = []
        for h in range(N_SB_HEADS):
            later = _dot(tri2, hl_ref[h])
            w = jnp.exp2(lb_ref[h] + later + run[h])
            if visible is not None:
                w = jnp.where(visible, w, 0.0)
            vt = vt_ref[0, h * HEAD_DIM:(h + 1) * HEAD_DIM, pl.ds(start, tk)]
            acc_ref[h] += _dot(vt, w.astype(BF16))
            new_run.append(run[h] + later[0:1, :] + newest[h])
        return new_run

    run = block(2 * i + 1, [jnp.zeros((1, tq), F32)] * N_SB_HEADS, tk)
    run = block(2 * i, run, 0)

    def live(run):
        m = run[0]
        for r in run[1:]:
            m = jnp.maximum(m, r)
        return jnp.max(m) > SB_LOG2_CUTOFF

    def cond(carry):
        j, go = carry[0], carry[1]
        return jnp.logical_and(j >= 0, go)

    def body(carry):
        j = carry[0]
        run = block(j, list(carry[2:]), None)
        return (j - 1, live(run), *run)

    lax.while_loop(cond, body, (2 * i - 1, live(run), *run))

    for p in range(N_SB_HEADS // PAIR):
        out_t = jnp.concatenate([acc_ref[PAIR * p + h] for h in range(PAIR)], axis=0)
        o_ref[0, :, p * LANES:(p + 1) * LANES] = out_t.T.astype(o_ref.dtype)


def _sb_attention(qt, k, vt):
    batch, seq, width = k.shape
    tq, tk = SB_TQ, SB_TK
    return pl.pallas_call(
        _sb_kernel,
        out_shape=jax.ShapeDtypeStruct((batch, seq, width), BF16),
        grid=(batch, seq // tq),
        in_specs=[
            pl.BlockSpec((1, width, tq), lambda b, i: (b, 0, i)),
            pl.BlockSpec((1, seq, width), lambda b, i: (b, 0, 0)),
            pl.BlockSpec((1, width, seq), lambda b, i: (b, 0, 0)),
        ],
        out_specs=pl.BlockSpec((1, tq, width), lambda b, i: (b, i, 0)),
        scratch_shapes=[pltpu.VMEM((N_SB_HEADS, tk, tq), F32),
                        pltpu.VMEM((N_SB_HEADS, tk, tq), F32),
                        pltpu.VMEM((N_SB_HEADS, 2 * tk, tq), BF16),
                        pltpu.VMEM((N_SB_HEADS, HEAD_DIM, tq), F32)],
        compiler_params=_cparams(("arbitrary", "arbitrary")),
        name="sb",
    )(qt, k, vt)


FOX_TQ = 512
FOX_TK = 256
FOX_ONES = 16
NEG_BIG = -1e30


def _fox_kernel(qt_ref, k_ref, faug_ref, vt_ref, o_ref, s_ref, acc_ref):
    tq, tk = FOX_TQ, FOX_TK
    pair = pl.program_id(1)
    i = pl.program_id(2)
    key = lax.broadcasted_iota(jnp.int32, (tk, tq), 0)
    qry = lax.broadcasted_iota(jnp.int32, (tk, tq), 1)
    dim = lax.broadcasted_iota(jnp.int32, (LANES, tq), 0)

    qt = qt_ref[0].astype(F32)
    q_aug = []
    for h in range(PAIR):
        own = jnp.where(dim // HEAD_DIM == h, qt, 0.0)
        bias_rows = jnp.where(dim // 3 == PAIR * pair + h, 1.0, 0.0)
        q_aug.append(jnp.concatenate([own, bias_rows], axis=0).astype(BF16))
    ones = jnp.ones((FOX_ONES, tk), BF16)
    acc_ref[...] = jnp.zeros_like(acc_ref)

    def scores(j, slot):
        start = pl.multiple_of(j * tk, tk)
        k_aug = jnp.concatenate([k_ref[0, pl.ds(start, tk), :], faug_ref[0, pl.ds(start, tk), :]], axis=1)
        for h in range(PAIR):
            s_ref[slot, h] = _dot(k_aug, q_aug[h])

    def update(j, slot, m, key_offset):
        start = pl.multiple_of(j * tk, tk)
        m_out = []
        for h in range(PAIR):
            s2 = s_ref[slot, h]
            if key_offset is not None:
                s2 = jnp.where(key + key_offset <= qry, s2, NEG_BIG)
            m_new = jnp.maximum(m[h], jnp.max(s2, axis=0, keepdims=True))
            alpha = jnp.exp2(m[h] - m_new)
            pr = jnp.exp2(s2 - m_new).astype(BF16)
            vt = jnp.concatenate([vt_ref[0, h * HEAD_DIM:(h + 1) * HEAD_DIM, pl.ds(start, tk)], ones], axis=0)
            acc_ref[h] = alpha * acc_ref[h] + _dot(vt, pr)
            m_out.append(m_new)
        return tuple(m_out)

    def body(jj, m):
        scores(2 * jj + 1, 1)
        m = update(2 * jj, 0, m, None)
        scores(2 * jj + 2, 0)
        return update(2 * jj + 1, 1, m, None)

    scores(0, 0)
    m = lax.fori_loop(0, i, body, (jnp.full((1, tq), NEG_BIG, F32),) * PAIR)
    scores(2 * i + 1, 1)
    m = update(2 * i, 0, m, 0)
    update(2 * i + 1, 1, m, tk)
    out_t = jnp.concatenate(
        [acc_ref[h, :HEAD_DIM, :] / acc_ref[h, HEAD_DIM:HEAD_DIM + 1, :] for h in range(PAIR)], axis=0)
    o_ref[0] = out_t.T.astype(o_ref.dtype)


def _fox_attention(qt, k, faug, vt):
    batch, seq, width = k.shape
    tq, tk = FOX_TQ, FOX_TK
    n_pairs = width // LANES
    return pl.pallas_call(
        _fox_kernel,
        out_shape=jax.ShapeDtypeStruct((batch, seq, width), BF16),
        grid=(batch, n_pairs, seq // tq),
        in_specs=[
            pl.BlockSpec((1, LANES, tq), lambda b, p, i: (b, p, i)),
            pl.BlockSpec((1, seq, LANES), lambda b, p, i: (b, 0, p)),
            pl.BlockSpec((1, seq, LANES), lambda b, p, i: (b, 0, 0)),
            pl.BlockSpec((1, LANES, seq), lambda b, p, i: (b, p, 0)),
        ],
        out_specs=pl.BlockSpec((1, tq, LANES), lambda b, p, i: (b, i, p)),
        scratch_shapes=[pltpu.VMEM((2, PAIR, tk, tq), F32),
                        pltpu.VMEM((PAIR, HEAD_DIM + FOX_ONES, tq), F32)],
        compiler_params=_cparams(("arbitrary", "arbitrary", "arbitrary")),
        name="fox",
    )(qt, k, faug, vt)


OUT_TM = 512


def _outproj_kernel(x_ref, sb_ref, fox_ref, mod_ref, wsb_ref, wfox_ref, o_ref):
    gate = mod_ref[0, 5:6, :]
    y = _dot(sb_ref[...], wsb_ref[...]) + _dot(fox_ref[...], wfox_ref[...])
    o_ref[...] = x_ref[...] + (1.0 + gate) * y


def _outproj(x2d, sb2d, fox2d, mod3, w_sb, w_fox, *, seq):
    t, d = x2d.shape
    tm = OUT_TM
    per_batch = seq // tm
    const = lambda i: (0, 0)
    tile = lambda i: (i, 0)
    return pl.pallas_call(
        _outproj_kernel,
        out_shape=jax.ShapeDtypeStruct((t, d), F32),
        grid=(t // tm,),
        in_specs=[
            pl.BlockSpec((tm, d), tile),
            pl.BlockSpec((tm, SB_WIDTH), tile),
            pl.BlockSpec((tm, FOX_WIDTH), tile),
            pl.BlockSpec((1, N_MOD, d), lambda i: (i // per_batch, 0, 0)),
            pl.BlockSpec(w_sb.shape, const),
            pl.BlockSpec(w_fox.shape, const),
        ],
        out_specs=pl.BlockSpec((tm, d), tile),
        compiler_params=_cparams(("arbitrary",)),
        name="outproj",
    )(x2d, sb2d, fox2d, mod3, w_sb, w_fox)


def kernel(x, c, w_mod, b_mod, g_ffn1, w1_gate, w1_up, w1_down, g_mix, w_in, b_f, g_q, g_k,
           w_o, g_ffn2, w2_gate, w2_up, w2_down):
    batch, seq, d = x.shape
    depth = w_mod.shape[0]
    qkv_width = 3 * SB_WIDTH + 3 * FOX_WIDTH
    xf = x.reshape(batch * seq, d)
    for l in range(depth):
        mod3 = _modulation(c, w_mod[l], b_mod[l]).reshape(batch, N_MOD, d)
        xf = _ffn(xf, mod3, g_ffn1[l].reshape(1, d), w1_gate[l].astype(BF16), w1_up[l].astype(BF16),
                  w1_down[l].astype(BF16), mod_base=0, seq=seq)

        w_main = w_in[l][:, :qkv_width].astype(BF16)
        gate_pad = LANES - 3 * N_FOX_HEADS
        w_f = jnp.pad(jnp.repeat(w_in[l][:, qkv_width:], 3, axis=1), ((0, 0), (0, gate_pad))).astype(BF16)
        b_f_pad = jnp.pad(jnp.repeat(b_f[l], 3), (0, gate_pad)).reshape(1, LANES)
        sbqt, sbk, sbvt, fqt, fk, fvt, faug = _inproj(
            xf.reshape(batch, seq, d), mod3, g_mix[l].reshape(1, d), w_main, w_f, b_f_pad,
            g_q[l].reshape(1, FOX_WIDTH), g_k[l].reshape(1, FOX_WIDTH))
        sb_out = _sb_attention(sbqt, sbk, sbvt)
        fox_out = _fox_attention(fqt, fk, faug, fvt)
        w_o_b = w_o[l].astype(BF16)
        xf = _outproj(xf, sb_out.reshape(batch * seq, SB_WIDTH), fox_out.reshape(batch * seq, FOX_WIDTH),
                      mod3, w_o_b[:SB_WIDTH], w_o_b[SB_WIDTH:], seq=seq)

        xf = _ffn(xf, mod3, g_ffn2[l].reshape(1, d), w2_gate[l].astype(BF16), w2_up[l].astype(BF16),
                  w2_down[l].astype(BF16), mod_base=6, seq=seq)
    return xf.reshape(batch, seq, d)
```

```python
import functools

import jax
import jax.numpy as jnp
from jax import lax
from jax.experimental import pallas as pl
from jax.experimental.pallas import tpu as pltpu

F32 = jnp.float32
BF16 = jnp.bfloat16

HEAD_DIM = 64
N_SB_HEADS = 8
N_FOX_HEADS = 8
SB_WIDTH = N_SB_HEADS * HEAD_DIM
FOX_WIDTH = N_FOX_HEADS * HEAD_DIM
N_MOD = 9
EPS = 1e-6
LOG2E = 1.4426950408889634
LANES = 128
PAIR = LANES // HEAD_DIM

VMEM_LIMIT = 56 * 1024 * 1024

SB_LOG2_CUTOFF = -60.0 * LOG2E


def _cparams(sem):
    return pltpu.CompilerParams(dimension_semantics=sem, vmem_limit_bytes=VMEM_LIMIT)


def _log_sigmoid(x):
    return jnp.minimum(x, 0.0) - jnp.log1p(jnp.exp(-jnp.abs(x)))


def _split2(x):
    hi = x.astype(BF16)
    lo = (x - hi.astype(F32)).astype(BF16)
    return hi, lo


def _split3(x):
    hi = x.astype(BF16)
    r = x - hi.astype(F32)
    mid = r.astype(BF16)
    lo = (r - mid.astype(F32)).astype(BF16)
    return hi, mid, lo


def _dot(a, b):
    return jnp.dot(a, b, preferred_element_type=F32)


def _dot_nt(a, b):
    return lax.dot_general(a, b, (((1,), (1,)), ((), ())), preferred_element_type=F32)


def _norm_modulate(x, g, shift, scale):
    ms = jnp.mean(x * x, axis=-1, keepdims=True)
    y = x * lax.rsqrt(ms + EPS) * g
    return y * (1.0 + scale) + shift


def _mod_kernel(c_ref, w_ref, b_ref, o_ref):
    c = c_ref[...]
    c_act = c * jax.nn.sigmoid(c)
    o_ref[...] = jnp.dot(c_act, w_ref[...], preferred_element_type=F32,
                         precision=lax.Precision.HIGHEST) + b_ref[...]


def _modulation(c, w_mod, b_mod):
    batch, d = c.shape
    n = w_mod.shape[1]
    tn = d
    return pl.pallas_call(
        _mod_kernel,
        out_shape=jax.ShapeDtypeStruct((batch, n), F32),
        grid=(n // tn,),
        in_specs=[
            pl.BlockSpec((batch, d), lambda j: (0, 0)),
            pl.BlockSpec((d, tn), lambda j: (0, j)),
            pl.BlockSpec((1, tn), lambda j: (0, j)),
        ],
        out_specs=pl.BlockSpec((batch, tn), lambda j: (0, j)),
        compiler_params=_cparams(("arbitrary",)),
        name="mod",
    )(c, w_mod, b_mod.reshape(1, n))


FFN_TM = 1024
FFN_SUB = 512
FFN_CHUNK = 256


def _ffn_kernel(*refs, mod_base, mix_gate_row):
    if mix_gate_row is None:
        x_ref, mod_ref, g_ref, wg_ref, wu_ref, wd_ref, o_ref, h_ref = refs
    else:
        x_ref, sb_ref, fox_ref, wsb_ref, wfox_ref, mod_ref, g_ref, wg_ref, wu_ref, wd_ref, o_ref, h_ref = refs
    shift = mod_ref[0, mod_base:mod_base + 1, :]
    scale = mod_ref[0, mod_base + 1:mod_base + 2, :]
    gate = mod_ref[0, mod_base + 2:mod_base + 3, :]
    d_ff = wg_ref.shape[1]
    for s in range(x_ref.shape[0] // FFN_SUB):
        rows = slice(s * FFN_SUB, (s + 1) * FFN_SUB)
        x = x_ref[rows, :]
        if mix_gate_row is not None:
            mixed = _dot(sb_ref[rows, :], wsb_ref[...]) + _dot(fox_ref[rows, :], wfox_ref[...])
            x = x + (1.0 + mod_ref[0, mix_gate_row:mix_gate_row + 1, :]) * mixed
        hb = _norm_modulate(x, g_ref[...], shift, scale).astype(BF16)
        for c in range(d_ff // FFN_CHUNK):
            cols = slice(c * FFN_CHUNK, (c + 1) * FFN_CHUNK)
            a = _dot(hb, wg_ref[:, cols])
            u = _dot(hb, wu_ref[:, cols])
            h_ref[rows, cols] = (a * jax.nn.sigmoid(a) * u).astype(BF16)
        y = _dot(h_ref[rows, :], wd_ref[...])
        o_ref[rows, :] = x + (0.5 * (1.0 + gate)) * y


def _ffn(x2d, mod3, g, wg, wu, wd, *, mod_base, seq, mix=None):
    t, d = x2d.shape
    d_ff = wg.shape[1]
    tm = FFN_TM
    per_batch = seq // tm
    const = lambda i: (0, 0)
    tile = lambda i: (i, 0)
    operands = [x2d]
    in_specs = [pl.BlockSpec((tm, d), tile)]
    if mix is not None:
        sb2d, fox2d, w_sb, w_fox, _ = mix
        operands += [sb2d, fox2d, w_sb, w_fox]
        in_specs += [pl.BlockSpec((tm, sb2d.shape[1]), tile), pl.BlockSpec((tm, fox2d.shape[1]), tile),
                     pl.BlockSpec(w_sb.shape, const), pl.BlockSpec(w_fox.shape, const)]
    operands += [mod3, g, wg, wu, wd]
    in_specs += [
        pl.BlockSpec((1, N_MOD, d), lambda i: (i // per_batch, 0, 0)),
        pl.BlockSpec((1, d), const),
        pl.BlockSpec((d, d_ff), const),
        pl.BlockSpec((d, d_ff), const),
        pl.BlockSpec((d_ff, d), const),
    ]
    return pl.pallas_call(
        functools.partial(_ffn_kernel, mod_base=mod_base, mix_gate_row=None if mix is None else mix[4]),
        out_shape=jax.ShapeDtypeStruct((t, d), F32),
        grid=(t // tm,),
        in_specs=in_specs,
        out_specs=pl.BlockSpec((tm, d), tile),
        scratch_shapes=[pltpu.VMEM((tm, d_ff), BF16)],
        compiler_params=_cparams(("arbitrary",)),
        name=f"ffn{mod_base}",
    )(*operands)


PROJ_TM = 512
NORM_GROUP = 256
CUM_ROWS = 256


def _head_group_sumsq(y):
    r = lax.broadcasted_iota(jnp.int32, (NORM_GROUP, NORM_GROUP), 0) // HEAD_DIM
    c = lax.broadcasted_iota(jnp.int32, (NORM_GROUP, NORM_GROUP), 1) // HEAD_DIM
    bd = jnp.where(r == c, 1.0, 0.0).astype(BF16)
    sq = (y * y).astype(BF16)
    parts = []
    for g in range(y.shape[1] // NORM_GROUP):
        parts.append(_dot(sq[:, g * NORM_GROUP:(g + 1) * NORM_GROUP], bd))
    return jnp.concatenate(parts, axis=1)


def _inproj_kernel(x_ref, mod_ref, g_ref, w_ref, wf_ref, bf_ref, gq_ref, gk_ref,
                   sbq_ref, sbk_ref, sbv_ref, fqt_ref, fk_ref, fvt_ref, faug_ref, carry_ref):
    tm = x_ref.shape[1]
    x = x_ref[0]
    shift = mod_ref[0, 3:4, :]
    scale = mod_ref[0, 4:5, :]
    hb = _norm_modulate(x, g_ref[...], shift, scale).astype(BF16)

    def proj(c):
        return _dot(hb, w_ref[:, c * SB_WIDTH:(c + 1) * SB_WIDTH])

    qk_scale = HEAD_DIM ** -0.5
    sbq_ref[0] = (proj(0) * (qk_scale * LOG2E)).T.astype(BF16)
    sbk_ref[0] = proj(1).astype(BF16)
    sbv_ref[0] = proj(2).T.astype(BF16)

    y = proj(3)
    ms = _head_group_sumsq(y) * (1.0 / HEAD_DIM)
    fq = y * lax.rsqrt(ms + EPS) * gq_ref[...] * (qk_scale * LOG2E)
    fqt_ref[0] = fq.T.astype(BF16)
    y = proj(4)
    ms = _head_group_sumsq(y) * (1.0 / HEAD_DIM)
    fk_ref[0] = (y * lax.rsqrt(ms + EPS) * gk_ref[...]).astype(BF16)
    fvt_ref[0] = proj(5).T.astype(BF16)

    @pl.when(pl.program_id(1) == 0)
    def _():
        carry_ref[...] = jnp.zeros_like(carry_ref)

    logf = _log_sigmoid(_dot(hb, wf_ref[...]) + bf_ref[...])
    r = lax.broadcasted_iota(jnp.int32, (CUM_ROWS, CUM_ROWS), 0)
    c = lax.broadcasted_iota(jnp.int32, (CUM_ROWS, CUM_ROWS), 1)
    tri = jnp.where(r >= c, 1.0, 0.0).astype(BF16)
    hi, mid, lo = _split3(logf)
    carry = carry_ref[...]
    parts = []
    for s in range(tm // CUM_ROWS):
        rows = slice(s * CUM_ROWS, (s + 1) * CUM_ROWS)
        part = _dot(tri, hi[rows]) + _dot(tri, mid[rows]) + _dot(tri, lo[rows]) + carry
        carry = part[CUM_ROWS - 1:CUM_ROWS, :]
        parts.append(part)
    cum = jnp.concatenate(parts, axis=0)
    carry_ref[...] = carry
    hi, mid, lo = _split3(cum * (-LOG2E))
    lane = lax.broadcasted_iota(jnp.int32, (1, LANES), 1)
    piece = jnp.where(lane % 3 == 0, hi, jnp.where(lane % 3 == 1, mid, lo))
    faug_ref[0] = jnp.where(lane < 3 * N_FOX_HEADS, piece, jnp.zeros_like(piece))


def _inproj(x3, mod3, g, w_main, w_f, b_f, gq, gk):
    batch, seq, d = x3.shape
    tm = PROJ_TM
    const2 = lambda b, i: (0, 0)
    tile = lambda b, i: (b, i, 0)
    row_major = jax.ShapeDtypeStruct((batch, seq, SB_WIDTH), BF16)
    transposed = jax.ShapeDtypeStruct((batch, SB_WIDTH, seq), BF16)
    rm_spec = pl.BlockSpec((1, tm, SB_WIDTH), tile)
    t_spec = pl.BlockSpec((1, SB_WIDTH, tm), lambda b, i: (b, 0, i))
    return pl.pallas_call(
        _inproj_kernel,
        out_shape=(transposed, row_major, transposed, transposed, row_major, transposed,
                   jax.ShapeDtypeStruct((batch, seq, LANES), BF16)),
        grid=(batch, seq // tm),
        in_specs=[
            pl.BlockSpec((1, tm, d), tile),
            pl.BlockSpec((1, N_MOD, d), lambda b, i: (b, 0, 0)),
            pl.BlockSpec((1, d), const2),
            pl.BlockSpec(w_main.shape, const2),
            pl.BlockSpec(w_f.shape, const2),
            pl.BlockSpec((1, LANES), const2),
            pl.BlockSpec((1, FOX_WIDTH), const2),
            pl.BlockSpec((1, FOX_WIDTH), const2),
        ],
        out_specs=(t_spec, rm_spec, t_spec, t_spec, rm_spec, t_spec,
                   pl.BlockSpec((1, tm, LANES), tile)),
        scratch_shapes=[pltpu.VMEM((1, LANES), F32)],
        compiler_params=_cparams(("arbitrary", "arbitrary")),
        name="inproj",
    )(x3, mod3, g, w_main, w_f, b_f, gq, gk)


SB_T = 128
SB_PAIRS = N_SB_HEADS // PAIR


def _sb_kernel(qt_ref, k_ref, vt_ref, o_ref, sc_ref, lb_ref, hl_ref, acc_ref):
    t = SB_T
    i = pl.program_id(1)
    key = lax.broadcasted_iota(jnp.int32, (t, PAIR * t), 0)
    qry = lax.broadcasted_iota(jnp.int32, (t, PAIR * t), 1) & (t - 1)
    visible = key < qry
    older = lax.broadcasted_iota(jnp.int32, (t, t), 0)
    newer = lax.broadcasted_iota(jnp.int32, (t, t), 1)
    tri = jnp.where(newer > older, 1.0, 0.0).astype(BF16)
    tri2 = jnp.concatenate([tri, tri], axis=1)
    dim = lax.broadcasted_iota(jnp.int32, (LANES, t), 0)

    q_pair = []
    for p in range(SB_PAIRS):
        qt = qt_ref[0, p * LANES:(p + 1) * LANES, :].astype(F32)
        q_pair.append(jnp.concatenate(
            [jnp.where(dim // HEAD_DIM == h, qt, 0.0) for h in range(PAIR)], axis=1).astype(BF16))
    acc_ref[...] = jnp.zeros_like(acc_ref)

    def scores(j):
        start = pl.multiple_of(j * t, t)
        for p in range(SB_PAIRS):
            sc_ref[p] = _dot(k_ref[0, pl.ds(start, t), p * LANES:(p + 1) * LANES], q_pair[p])

    def block(j, run, diagonal):
        start = pl.multiple_of(j * t, t)
        newest = []
        for p in range(SB_PAIRS):
            z = sc_ref[p]
            soft = jnp.log2(1.0 + jnp.exp2(-jnp.abs(z)))
            log_beta = jnp.minimum(z, 0.0) - soft
            log_keep = log_beta - z
            if diagonal:
                log_keep = jnp.where(visible, log_keep, 0.0)
            lb_ref[p] = log_beta + run[p]
            hi, lo = _split2(log_keep)
            hl_ref[p, :t, :] = hi
            hl_ref[p, t:, :] = lo
            newest.append(log_keep[0:1, :])
        scores(jnp.maximum(j - 1, 0))
        new_run = []
        for p in range(SB_PAIRS):
            later = _dot(tri2, hl_ref[p])
            w = jnp.exp2(lb_ref[p] + later)
            if diagonal:
                w = jnp.where(visible, w, 0.0)
            acc_ref[p] += _dot(vt_ref[0, p * LANES:(p + 1) * LANES, pl.ds(start, t)], w.astype(BF16))
            new_run.append(run[p] + later[0:1, :] + newest[p])
        return new_run

    scores(i)
    run = block(i, [jnp.zeros((1, PAIR * t), F32)] * SB_PAIRS, True)

    def live(run):
        m = run[0]
        for r in run[1:]:
            m = jnp.maximum(m, r)
        return jnp.max(m) > SB_LOG2_CUTOFF

    def cond(carry):
        j, go = carry[0], carry[1]
        return jnp.logical_and(j >= 0, go)

    def body(carry):
        j = carry[0]
        run = block(j, list(carry[2:]), False)
        return (j - 1, live(run), *run)

    lax.while_loop(cond, body, (i - 1, live(run), *run))

    for p in range(SB_PAIRS):
        out_t = jnp.concatenate(
            [acc_ref[p, h * HEAD_DIM:(h + 1) * HEAD_DIM, h * t:(h + 1) * t] for h in range(PAIR)], axis=0)
        o_ref[0, :, p * LANES:(p + 1) * LANES] = out_t.T.astype(o_ref.dtype)


def _sb_attention(qt, k, vt):
    batch, seq, width = k.shape
    t = SB_T
    return pl.pallas_call(
        _sb_kernel,
        out_shape=jax.ShapeDtypeStruct((batch, seq, width), BF16),
        grid=(batch, seq // t),
        in_specs=[
            pl.BlockSpec((1, width, t), lambda b, i: (b, 0, i)),
            pl.BlockSpec((1, seq, width), lambda b, i: (b, 0, 0)),
            pl.BlockSpec((1, width, seq), lambda b, i: (b, 0, 0)),
        ],
        out_specs=pl.BlockSpec((1, t, width), lambda b, i: (b, i, 0)),
        scratch_shapes=[pltpu.VMEM((SB_PAIRS, t, PAIR * t), F32),
                        pltpu.VMEM((SB_PAIRS, t, PAIR * t), F32),
                        pltpu.VMEM((SB_PAIRS, 2 * t, PAIR * t), BF16),
                        pltpu.VMEM((SB_PAIRS, LANES, PAIR * t), F32)],
        compiler_params=_cparams(("arbitrary", "arbitrary")),
        name="sb",
    )(qt, k, vt)


FOX_TQ = 512
FOX_TK = 256
FOX_ONES = 16
NEG_BIG = -1e30


def _fox_kernel(qt_ref, k_ref, faug_ref, vt_ref, o_ref, s_ref, acc_ref):
    tq, tk = FOX_TQ, FOX_TK
    pair = pl.program_id(1)
    i = pl.program_id(2)
    key = lax.broadcasted_iota(jnp.int32, (tk, tq), 0)
    qry = lax.broadcasted_iota(jnp.int32, (tk, tq), 1)
    dim = lax.broadcasted_iota(jnp.int32, (LANES, tq), 0)

    qt = qt_ref[0].astype(F32)
    q_aug = []
    for h in range(PAIR):
        own = jnp.where(dim // HEAD_DIM == h, qt, 0.0)
        bias_rows = jnp.where(dim // 3 == PAIR * pair + h, 1.0, 0.0)
        q_aug.append(jnp.concatenate([own, bias_rows], axis=0).astype(BF16))
    ones = jnp.ones((FOX_ONES, tk), BF16)
    acc_ref[...] = jnp.zeros_like(acc_ref)

    def scores(j, slot):
        start = pl.multiple_of(j * tk, tk)
        k_aug = jnp.concatenate([k_ref[0, pl.ds(start, tk), :], faug_ref[0, pl.ds(start, tk), :]], axis=1)
        for h in range(PAIR):
            s_ref[slot, h] = _dot(k_aug, q_aug[h])

    def update(j, slot, m, key_offset):
        start = pl.multiple_of(j * tk, tk)
        m_out = []
        for h in range(PAIR):
            s2 = s_ref[slot, h]
            if key_offset is not None:
                s2 = jnp.where(key + key_offset <= qry, s2, NEG_BIG)
            m_new = jnp.maximum(m[h], jnp.max(s2, axis=0, keepdims=True))
            alpha = jnp.exp2(m[h] - m_new)
            pr = jnp.exp2(s2 - m_new).astype(BF16)
            vt = jnp.concatenate([vt_ref[0, h * HEAD_DIM:(h + 1) * HEAD_DIM, pl.ds(start, tk)], ones], axis=0)
            acc_ref[h] = alpha * acc_ref[h] + _dot(vt, pr)
            m_out.append(m_new)
        return tuple(m_out)

    def body(jj, m):
        scores(2 * jj + 1, 1)
        m = update(2 * jj, 0, m, None)
        scores(2 * jj + 2, 0)
        return update(2 * jj + 1, 1, m, None)

    scores(0, 0)
    m = lax.fori_loop(0, i, body, (jnp.full((1, tq), NEG_BIG, F32),) * PAIR)
    scores(2 * i + 1, 1)
    m = update(2 * i, 0, m, 0)
    update(2 * i + 1, 1, m, tk)
    out_t = jnp.concatenate(
        [acc_ref[h, :HEAD_DIM, :] / acc_ref[h, HEAD_DIM:HEAD_DIM + 1, :] for h in range(PAIR)], axis=0)
    o_ref[0] = out_t.T.astype(o_ref.dtype)


def _fox_attention(qt, k, faug, vt):
    batch, seq, width = k.shape
    tq, tk = FOX_TQ, FOX_TK
    n_pairs = width // LANES
    return pl.pallas_call(
        _fox_kernel,
        out_shape=jax.ShapeDtypeStruct((batch, seq, width), BF16),
        grid=(batch, n_pairs, seq // tq),
        in_specs=[
            pl.BlockSpec((1, LANES, tq), lambda b, p, i: (b, p, i)),
            pl.BlockSpec((1, seq, LANES), lambda b, p, i: (b, 0, p)),
            pl.BlockSpec((1, seq, LANES), lambda b, p, i: (b, 0, 0)),
            pl.BlockSpec((1, LANES, seq), lambda b, p, i: (b, p, 0)),
        ],
        out_specs=pl.BlockSpec((1, tq, LANES), lambda b, p, i: (b, i, p)),
        scratch_shapes=[pltpu.VMEM((2, PAIR, tk, tq), F32),
                        pltpu.VMEM((PAIR, HEAD_DIM + FOX_ONES, tq), F32)],
        compiler_params=_cparams(("arbitrary", "arbitrary", "arbitrary")),
        name="fox",
    )(qt, k, faug, vt)


def kernel(x, c, w_mod, b_mod, g_ffn1, w1_gate, w1_up, w1_down, g_mix, w_in, b_f, g_q, g_k,
           w_o, g_ffn2, w2_gate, w2_up, w2_down):
    batch, seq, d = x.shape
    depth = w_mod.shape[0]
    qkv_width = 3 * SB_WIDTH + 3 * FOX_WIDTH
    xf = x.reshape(batch * seq, d)
    for l in range(depth):
        mod3 = _modulation(c, w_mod[l], b_mod[l]).reshape(batch, N_MOD, d)
        xf = _ffn(xf, mod3, g_ffn1[l].reshape(1, d), w1_gate[l].astype(BF16), w1_up[l].astype(BF16),
                  w1_down[l].astype(BF16), mod_base=0, seq=seq)

        w_main = w_in[l][:, :qkv_width].astype(BF16)
        gate_pad = LANES - 3 * N_FOX_HEADS
        w_f = jnp.pad(jnp.repeat(w_in[l][:, qkv_width:], 3, axis=1), ((0, 0), (0, gate_pad))).astype(BF16)
        b_f_pad = jnp.pad(jnp.repeat(b_f[l], 3), (0, gate_pad)).reshape(1, LANES)
        sbqt, sbk, sbvt, fqt, fk, fvt, faug = _inproj(
            xf.reshape(batch, seq, d), mod3, g_mix[l].reshape(1, d), w_main, w_f, b_f_pad,
            g_q[l].reshape(1, FOX_WIDTH), g_k[l].reshape(1, FOX_WIDTH))
        sb_out = _sb_attention(sbqt, sbk, sbvt)
        fox_out = _fox_attention(fqt, fk, faug, fvt)
        w_o_b = w_o[l].astype(BF16)
        mix = (sb_out.reshape(batch * seq, SB_WIDTH), fox_out.reshape(batch * seq, FOX_WIDTH),
               w_o_b[:SB_WIDTH], w_o_b[SB_WIDTH:], 5)
        xf = _ffn(xf, mod3, g_ffn2[l].reshape(1, d), w2_gate[l].astype(BF16), w2_up[l].astype(BF16),
                  w2_down[l].astype(BF16), mod_base=6, seq=seq, mix=mix)
    return xf.reshape(batch, seq, d)
```

```python
import functools

import jax
import jax.numpy as jnp
from jax import lax
from jax.experimental import pallas as pl
from jax.experimental.pallas import tpu as pltpu

F32 = jnp.float32
BF16 = jnp.bfloat16

HEAD_DIM = 64
N_SB_HEADS = 8
N_FOX_HEADS = 8
SB_WIDTH = N_SB_HEADS * HEAD_DIM
FOX_WIDTH = N_FOX_HEADS * HEAD_DIM
N_MOD = 9
EPS = 1e-6
LOG2E = 1.4426950408889634
LANES = 128
PAIR = LANES // HEAD_DIM

VMEM_LIMIT = 56 * 1024 * 1024

SB_LOG2_CUTOFF = -60.0 * LOG2E


def _cparams(sem):
    return pltpu.CompilerParams(dimension_semantics=sem, vmem_limit_bytes=VMEM_LIMIT)


def _log_sigmoid(x):
    return jnp.minimum(x, 0.0) - jnp.log1p(jnp.exp(-jnp.abs(x)))


def _split2(x):
    hi = x.astype(BF16)
    lo = (x - hi.astype(F32)).astype(BF16)
    return hi, lo


def _split3(x):
    hi = x.astype(BF16)
    r = x - hi.astype(F32)
    mid = r.astype(BF16)
    lo = (r - mid.astype(F32)).astype(BF16)
    return hi, mid, lo


def _dot(a, b):
    return jnp.dot(a, b, preferred_element_type=F32)


def _dot_nt(a, b):
    return lax.dot_general(a, b, (((1,), (1,)), ((), ())), preferred_element_type=F32)


def _norm_modulate(x, g, shift, scale):
    ms = jnp.mean(x * x, axis=-1, keepdims=True)
    y = x * lax.rsqrt(ms + EPS) * g
    return y * (1.0 + scale) + shift


def _mod_kernel(c_ref, w_ref, b_ref, o_ref):
    c = c_ref[...]
    c_act = c * jax.nn.sigmoid(c)
    o_ref[...] = jnp.dot(c_act, w_ref[...], preferred_element_type=F32,
                         precision=lax.Precision.HIGHEST) + b_ref[...]


def _modulation(c, w_mod, b_mod):
    batch, d = c.shape
    n = w_mod.shape[1]
    tn = d
    return pl.pallas_call(
        _mod_kernel,
        out_shape=jax.ShapeDtypeStruct((batch, n), F32),
        grid=(n // tn,),
        in_specs=[
            pl.BlockSpec((batch, d), lambda j: (0, 0)),
            pl.BlockSpec((d, tn), lambda j: (0, j)),
            pl.BlockSpec((1, tn), lambda j: (0, j)),
        ],
        out_specs=pl.BlockSpec((batch, tn), lambda j: (0, j)),
        compiler_params=_cparams(("arbitrary",)),
        name="mod",
    )(c, w_mod, b_mod.reshape(1, n))


FFN_TM = 1024
FFN_SUB = 512
FFN_CHUNK = 256


def _ffn_kernel(*refs, mod_base, mix_gate_row):
    if mix_gate_row is None:
        x_ref, mod_ref, g_ref, wg_ref, wu_ref, wd_ref, o_ref, h_ref = refs
    else:
        x_ref, sb_ref, fox_ref, wsb_ref, wfox_ref, mod_ref, g_ref, wg_ref, wu_ref, wd_ref, o_ref, h_ref = refs
    shift = mod_ref[0, mod_base:mod_base + 1, :]
    scale = mod_ref[0, mod_base + 1:mod_base + 2, :]
    gate = mod_ref[0, mod_base + 2:mod_base + 3, :]
    d_ff = wg_ref.shape[1]
    for s in range(x_ref.shape[0] // FFN_SUB):
        rows = slice(s * FFN_SUB, (s + 1) * FFN_SUB)
        x = x_ref[rows, :]
        if mix_gate_row is not None:
            mixed = _dot(sb_ref[rows, :], wsb_ref[...]) + _dot(fox_ref[rows, :], wfox_ref[...])
            x = x + (1.0 + mod_ref[0, mix_gate_row:mix_gate_row + 1, :]) * mixed
        hb = _norm_modulate(x, g_ref[...], shift, scale).astype(BF16)
        for c in range(d_ff // FFN_CHUNK):
            cols = slice(c * FFN_CHUNK, (c + 1) * FFN_CHUNK)
            a = _dot(hb, wg_ref[:, cols])
            u = _dot(hb, wu_ref[:, cols])
            h_ref[rows, cols] = (a * jax.nn.sigmoid(a) * u).astype(BF16)
        y = _dot(h_ref[rows, :], wd_ref[...])
        o_ref[rows, :] = x + (0.5 * (1.0 + gate)) * y


def _ffn(x2d, mod3, g, wg, wu, wd, *, mod_base, seq, mix=None):
    t, d = x2d.shape
    d_ff = wg.shape[1]
    tm = FFN_TM
    per_batch = seq // tm
    const = lambda i: (0, 0)
    tile = lambda i: (i, 0)
    operands = [x2d]
    in_specs = [pl.BlockSpec((tm, d), tile)]
    if mix is not None:
        sb2d, fox2d, w_sb, w_fox, _ = mix
        operands += [sb2d, fox2d, w_sb, w_fox]
        in_specs += [pl.BlockSpec((tm, sb2d.shape[1]), tile), pl.BlockSpec((tm, fox2d.shape[1]), tile),
                     pl.BlockSpec(w_sb.shape, const), pl.BlockSpec(w_fox.shape, const)]
    operands += [mod3, g, wg, wu, wd]
    in_specs += [
        pl.BlockSpec((1, N_MOD, d), lambda i: (i // per_batch, 0, 0)),
        pl.BlockSpec((1, d), const),
        pl.BlockSpec((d, d_ff), const),
        pl.BlockSpec((d, d_ff), const),
        pl.BlockSpec((d_ff, d), const),
    ]
    return pl.pallas_call(
        functools.partial(_ffn_kernel, mod_base=mod_base, mix_gate_row=None if mix is None else mix[4]),
        out_shape=jax.ShapeDtypeStruct((t, d), F32),
        grid=(t // tm,),
        in_specs=in_specs,
        out_specs=pl.BlockSpec((tm, d), tile),
        scratch_shapes=[pltpu.VMEM((tm, d_ff), BF16)],
        compiler_params=_cparams(("arbitrary",)),
        name=f"ffn{mod_base}",
    )(*operands)


PROJ_TM = 512
NORM_GROUP = 256
CUM_ROWS = 256


def _head_group_sumsq(y):
    r = lax.broadcasted_iota(jnp.int32, (NORM_GROUP, NORM_GROUP), 0) // HEAD_DIM
    c = lax.broadcasted_iota(jnp.int32, (NORM_GROUP, NORM_GROUP), 1) // HEAD_DIM
    bd = jnp.where(r == c, 1.0, 0.0).astype(BF16)
    sq = (y * y).astype(BF16)
    parts = []
    for g in range(y.shape[1] // NORM_GROUP):
        parts.append(_dot(sq[:, g * NORM_GROUP:(g + 1) * NORM_GROUP], bd))
    return jnp.concatenate(parts, axis=1)


def _inproj_kernel(x_ref, mod_ref, g_ref, w_ref, wf_ref, bf_ref, gq_ref, gk_ref,
                   sbq_ref, sbk_ref, sbv_ref, fqt_ref, fk_ref, fvt_ref, faug_ref, carry_ref):
    tm = x_ref.shape[1]
    x = x_ref[0]
    shift = mod_ref[0, 3:4, :]
    scale = mod_ref[0, 4:5, :]
    hb = _norm_modulate(x, g_ref[...], shift, scale).astype(BF16)

    def proj(c):
        return _dot(hb, w_ref[:, c * SB_WIDTH:(c + 1) * SB_WIDTH])

    qk_scale = HEAD_DIM ** -0.5
    sbq_ref[0] = (proj(0) * (qk_scale * LOG2E)).T.astype(BF16)
    sbk_ref[0] = proj(1).astype(BF16)
    sbv_ref[0] = proj(2).T.astype(BF16)

    y = proj(3)
    ms = _head_group_sumsq(y) * (1.0 / HEAD_DIM)
    fq = y * lax.rsqrt(ms + EPS) * gq_ref[...] * (qk_scale * LOG2E)
    fqt_ref[0] = fq.T.astype(BF16)
    y = proj(4)
    ms = _head_group_sumsq(y) * (1.0 / HEAD_DIM)
    fk_ref[0] = (y * lax.rsqrt(ms + EPS) * gk_ref[...]).astype(BF16)
    fvt_ref[0] = proj(5).T.astype(BF16)

    @pl.when(pl.program_id(1) == 0)
    def _():
        carry_ref[...] = jnp.zeros_like(carry_ref)

    logf = _log_sigmoid(_dot(hb, wf_ref[...]) + bf_ref[...])
    r = lax.broadcasted_iota(jnp.int32, (CUM_ROWS, CUM_ROWS), 0)
    c = lax.broadcasted_iota(jnp.int32, (CUM_ROWS, CUM_ROWS), 1)
    tri = jnp.where(r >= c, 1.0, 0.0).astype(BF16)
    hi, mid, lo = _split3(logf)
    carry = carry_ref[...]
    parts = []
    for s in range(tm // CUM_ROWS):
        rows = slice(s * CUM_ROWS, (s + 1) * CUM_ROWS)
        part = _dot(tri, hi[rows]) + _dot(tri, mid[rows]) + _dot(tri, lo[rows]) + carry
        carry = part[CUM_ROWS - 1:CUM_ROWS, :]
        parts.append(part)
    cum = jnp.concatenate(parts, axis=0)
    carry_ref[...] = carry
    hi, mid, lo = _split3(cum * (-LOG2E))
    lane = lax.broadcasted_iota(jnp.int32, (1, LANES), 1)
    piece = jnp.where(lane % 3 == 0, hi, jnp.where(lane % 3 == 1, mid, lo))
    faug_ref[0] = jnp.where(lane < 3 * N_FOX_HEADS, piece, jnp.zeros_like(piece))


def _inproj(x3, mod3, g, w_main, w_f, b_f, gq, gk):
    batch, seq, d = x3.shape
    tm = PROJ_TM
    const2 = lambda b, i: (0, 0)
    tile = lambda b, i: (b, i, 0)
    row_major = jax.ShapeDtypeStruct((batch, seq, SB_WIDTH), BF16)
    transposed = jax.ShapeDtypeStruct((batch, SB_WIDTH, seq), BF16)
    rm_spec = pl.BlockSpec((1, tm, SB_WIDTH), tile)
    t_spec = pl.BlockSpec((1, SB_WIDTH, tm), lambda b, i: (b, 0, i))
    return pl.pallas_call(
        _inproj_kernel,
        out_shape=(transposed, row_major, transposed, transposed, row_major, transposed,
                   jax.ShapeDtypeStruct((batch, seq, LANES), BF16)),
        grid=(batch, seq // tm),
        in_specs=[
            pl.BlockSpec((1, tm, d), tile),
            pl.BlockSpec((1, N_MOD, d), lambda b, i: (b, 0, 0)),
            pl.BlockSpec((1, d), const2),
            pl.BlockSpec(w_main.shape, const2),
            pl.BlockSpec(w_f.shape, const2),
            pl.BlockSpec((1, LANES), const2),
            pl.BlockSpec((1, FOX_WIDTH), const2),
            pl.BlockSpec((1, FOX_WIDTH), const2),
        ],
        out_specs=(t_spec, rm_spec, t_spec, t_spec, rm_spec, t_spec,
                   pl.BlockSpec((1, tm, LANES), tile)),
        scratch_shapes=[pltpu.VMEM((1, LANES), F32)],
        compiler_params=_cparams(("arbitrary", "arbitrary")),
        name="inproj",
    )(x3, mod3, g, w_main, w_f, b_f, gq, gk)


SB_T = 128
SB_PAIRS = N_SB_HEADS // PAIR


def _sb_kernel(qt_ref, k_ref, vt_ref, o_ref, sc_ref, lb_ref, hl_ref, acc_ref):
    t = SB_T
    i = pl.program_id(1)
    key = lax.broadcasted_iota(jnp.int32, (t, PAIR * t), 0)
    qry = lax.broadcasted_iota(jnp.int32, (t, PAIR * t), 1) & (t - 1)
    visible = key < qry
    older = lax.broadcasted_iota(jnp.int32, (t, t), 0)
    newer = lax.broadcasted_iota(jnp.int32, (t, t), 1)
    tri = jnp.where(newer > older, 1.0, 0.0).astype(BF16)
    tri2 = jnp.concatenate([tri, tri], axis=1)
    dim = lax.broadcasted_iota(jnp.int32, (LANES, t), 0)

    q_pair = []
    for p in range(SB_PAIRS):
        qt = qt_ref[0, p * LANES:(p + 1) * LANES, :].astype(F32)
        q_pair.append(jnp.concatenate(
            [jnp.where(dim // HEAD_DIM == h, qt, 0.0) for h in range(PAIR)], axis=1).astype(BF16))
    acc_ref[...] = jnp.zeros_like(acc_ref)

    def scores(j):
        start = pl.multiple_of(j * t, t)
        for p in range(SB_PAIRS):
            sc_ref[p] = _dot(k_ref[0, pl.ds(start, t), p * LANES:(p + 1) * LANES], q_pair[p])

    def block(j, run, diagonal):
        start = pl.multiple_of(j * t, t)
        newest = []
        for p in range(SB_PAIRS):
            z = sc_ref[p]
            soft = jnp.log2(1.0 + jnp.exp2(-jnp.abs(z)))
            log_beta = jnp.minimum(z, 0.0) - soft
            log_keep = log_beta - z
            if diagonal:
                log_keep = jnp.where(visible, log_keep, 0.0)
            lb_ref[p] = log_beta + run[p]
            hi, lo = _split2(log_keep)
            hl_ref[p, :t, :] = hi
            hl_ref[p, t:, :] = lo
            newest.append(log_keep[0:1, :])
        scores(jnp.maximum(j - 1, 0))
        new_run = []
        for p in range(SB_PAIRS):
            later = _dot(tri2, hl_ref[p])
            w = jnp.exp2(lb_ref[p] + later)
            if diagonal:
                w = jnp.where(visible, w, 0.0)
            acc_ref[p] += _dot(vt_ref[0, p * LANES:(p + 1) * LANES, pl.ds(start, t)], w.astype(BF16))
            new_run.append(run[p] + later[0:1, :] + newest[p])
        return new_run

    scores(i)
    run = block(i, [jnp.zeros((1, PAIR * t), F32)] * SB_PAIRS, True)

    def live(run):
        m = run[0]
        for r in run[1:]:
            m = jnp.maximum(m, r)
        return jnp.max(m) > SB_LOG2_CUTOFF

    def cond(carry):
        j, go = carry[0], carry[1]
        return jnp.logical_and(j >= 0, go)

    def body(carry):
        j = carry[0]
        run = block(j, list(carry[2:]), False)
        return (j - 1, live(run), *run)

    lax.while_loop(cond, body, (i - 1, live(run), *run))

    for p in range(SB_PAIRS):
        out_t = jnp.concatenate(
            [acc_ref[p, h * HEAD_DIM:(h + 1) * HEAD_DIM, h * t:(h + 1) * t] for h in range(PAIR)], axis=0)
        o_ref[0, :, p * LANES:(p + 1) * LANES] = out_t.T.astype(o_ref.dtype)


def _sb_attention(qt, k, vt):
    batch, seq, width = k.shape
    t = SB_T
    return pl.pallas_call(
        _sb_kernel,
        out_shape=jax.ShapeDtypeStruct((batch, seq, width), BF16),
        grid=(batch, seq // t),
        in_specs=[
            pl.BlockSpec((1, width, t), lambda b, i: (b, 0, i)),
            pl.BlockSpec((1, seq, width), lambda b, i: (b, 0, 0)),
            pl.BlockSpec((1, width, seq), lambda b, i: (b, 0, 0)),
        ],
        out_specs=pl.BlockSpec((1, t, width), lambda b, i: (b, i, 0)),
        scratch_shapes=[pltpu.VMEM((SB_PAIRS, t, PAIR * t), F32),
                        pltpu.VMEM((SB_PAIRS, t, PAIR * t), F32),
                        pltpu.VMEM((SB_PAIRS, 2 * t, PAIR * t), BF16),
                        pltpu.VMEM((SB_PAIRS, LANES, PAIR * t), F32)],
        compiler_params=_cparams(("arbitrary", "arbitrary")),
        name="sb",
    )(qt, k, vt)


FOX_TQ = 512
FOX_TK = 256
FOX_ONES = 16
NEG_BIG = -1e30


def _fox_kernel(qt_ref, k_ref, faug_ref, vt_ref, o_ref, s_ref, acc_ref):
    tq, tk = FOX_TQ, FOX_TK
    pair = pl.program_id(1)
    i = pl.program_id(2)
    key = lax.broadcasted_iota(jnp.int32, (tk, tq), 0)
    qry = lax.broadcasted_iota(jnp.int32, (tk, tq), 1)
    dim = lax.broadcasted_iota(jnp.int32, (LANES, tq), 0)

    qt = qt_ref[0].astype(F32)
    q_aug = []
    for h in range(PAIR):
        own = jnp.where(dim // HEAD_DIM == h, qt, 0.0)
        bias_rows = jnp.where(dim // 3 == PAIR * pair + h, 1.0, 0.0)
        q_aug.append(jnp.concatenate([own, bias_rows], axis=0).astype(BF16))
    ones = jnp.ones((FOX_ONES, tk), BF16)
    acc_ref[...] = jnp.zeros_like(acc_ref)

    def scores(j, slot):
        start = pl.multiple_of(j * tk, tk)
        k_aug = jnp.concatenate([k_ref[0, pl.ds(start, tk), :], faug_ref[0, pl.ds(start, tk), :]], axis=1)
        for h in range(PAIR):
            s_ref[slot, h] = _dot(k_aug, q_aug[h])

    def update(j, slot, m, key_offset):
        start = pl.multiple_of(j * tk, tk)
        m_out = []
        for h in range(PAIR):
            s2 = s_ref[slot, h]
            if key_offset is not None:
                s2 = jnp.where(key + key_offset <= qry, s2, NEG_BIG)
            m_new = jnp.maximum(m[h], jnp.max(s2, axis=0, keepdims=True))
            alpha = jnp.exp2(m[h] - m_new)
            pr = jnp.exp2(s2 - m_new).astype(BF16)
            vt = jnp.concatenate([vt_ref[0, h * HEAD_DIM:(h + 1) * HEAD_DIM, pl.ds(start, tk)], ones], axis=0)
            acc_ref[h] = alpha * acc_ref[h] + _dot(vt, pr)
            m_out.append(m_new)
        return tuple(m_out)

    def body(jj, m):
        scores(2 * jj + 1, 1)
        m = update(2 * jj, 0, m, None)
        scores(2 * jj + 2, 0)
        return update(2 * jj + 1, 1, m, None)

    def body_twice(kk, m):
        return body(2 * kk + 1, body(2 * kk, m))

    scores(0, 0)
    m = lax.fori_loop(0, i // 2, body_twice, (jnp.full((1, tq), NEG_BIG, F32),) * PAIR)
    m = lax.fori_loop(2 * (i // 2), i, body, m)
    scores(2 * i + 1, 1)
    m = update(2 * i, 0, m, 0)
    update(2 * i + 1, 1, m, tk)
    out_t = jnp.concatenate(
        [acc_ref[h, :HEAD_DIM, :] / acc_ref[h, HEAD_DIM:HEAD_DIM + 1, :] for h in range(PAIR)], axis=0)
    o_ref[0] = out_t.T.astype(o_ref.dtype)


def _fox_attention(qt, k, faug, vt):
    batch, seq, width = k.shape
    tq, tk = FOX_TQ, FOX_TK
    n_pairs = width // LANES
    return pl.pallas_call(
        _fox_kernel,
        out_shape=jax.ShapeDtypeStruct((batch, seq, width), BF16),
        grid=(batch, n_pairs, seq // tq),
        in_specs=[
            pl.BlockSpec((1, LANES, tq), lambda b, p, i: (b, p, i)),
            pl.BlockSpec((1, seq, LANES), lambda b, p, i: (b, 0, p)),
            pl.BlockSpec((1, seq, LANES), lambda b, p, i: (b, 0, 0)),
            pl.BlockSpec((1, LANES, seq), lambda b, p, i: (b, p, 0)),
        ],
        out_specs=pl.BlockSpec((1, tq, LANES), lambda b, p, i: (b, i, p)),
        scratch_shapes=[pltpu.VMEM((2, PAIR, tk, tq), F32),
                        pltpu.VMEM((PAIR, HEAD_DIM + FOX_ONES, tq), F32)],
        compiler_params=_cparams(("arbitrary", "arbitrary", "arbitrary")),
        name="fox",
    )(qt, k, faug, vt)


def kernel(x, c, w_mod, b_mod, g_ffn1, w1_gate, w1_up, w1_down, g_mix, w_in, b_f, g_q, g_k,
           w_o, g_ffn2, w2_gate, w2_up, w2_down):
    batch, seq, d = x.shape
    depth = w_mod.shape[0]
    qkv_width = 3 * SB_WIDTH + 3 * FOX_WIDTH
    xf = x.reshape(batch * seq, d)
    for l in range(depth):
        mod3 = _modulation(c, w_mod[l], b_mod[l]).reshape(batch, N_MOD, d)
        xf = _ffn(xf, mod3, g_ffn1[l].reshape(1, d), w1_gate[l].astype(BF16), w1_up[l].astype(BF16),
                  w1_down[l].astype(BF16), mod_base=0, seq=seq)

        w_main = w_in[l][:, :qkv_width].astype(BF16)
        gate_pad = LANES - 3 * N_FOX_HEADS
        w_f = jnp.pad(jnp.repeat(w_in[l][:, qkv_width:], 3, axis=1), ((0, 0), (0, gate_pad))).astype(BF16)
        b_f_pad = jnp.pad(jnp.repeat(b_f[l], 3), (0, gate_pad)).reshape(1, LANES)
        sbqt, sbk, sbvt, fqt, fk, fvt, faug = _inproj(
            xf.reshape(batch, seq, d), mod3, g_mix[l].reshape(1, d), w_main, w_f, b_f_pad,
            g_q[l].reshape(1, FOX_WIDTH), g_k[l].reshape(1, FOX_WIDTH))
        sb_out = _sb_attention(sbqt, sbk, sbvt)
        fox_out = _fox_attention(fqt, fk, faug, fvt)
        w_o_b = w_o[l].astype(BF16)
        mix = (sb_out.reshape(batch * seq, SB_WIDTH), fox_out.reshape(batch * seq, FOX_WIDTH),
               w_o_b[:SB_WIDTH], w_o_b[SB_WIDTH:], 5)
        xf = _ffn(xf, mod3, g_ffn2[l].reshape(1, d), w2_gate[l].astype(BF16), w2_up[l].astype(BF16),
                  w2_down[l].astype(BF16), mod_base=6, seq=seq, mix=mix)
    return xf.reshape(batch, seq, d)
```
